```python
import math
import jax
import jax.numpy as jnp
from jax import lax
import numpy as np

D_MODEL = 1024
BATCH = 16
SEQ = 2048
DEPTH = 1

PLE_DIM = 256
D_FF = 2816
SSM_GROUP_CH = 16
SSM_GROUPS = 32
D_SSM = SSM_GROUPS * SSM_GROUP_CH
SSM_STATE = 64
GMLP_HEADS = 8
GMLP_HEAD_DIM = 64
D_GMLP = GMLP_HEADS * GMLP_HEAD_DIM
CHUNK = 128
D_IN = D_SSM + 2 * D_GMLP + 2 * D_MODEL
LN_EPS = 1e-5
DEEPNORM_ALPHA = (2.0 * DEPTH) ** 0.25
DEEPNORM_BETA = (8.0 * DEPTH) ** -0.25

kernel_name = "hybrid_s5_gmlp_macaron_deepnorm"


def _layer_norm(x, g, b):
    xf = x.astype(jnp.float32)
    mu = jnp.mean(xf, axis=-1, keepdims=True)
    var = jnp.mean(jnp.square(xf - mu), axis=-1, keepdims=True)
    y = (xf - mu) * lax.rsqrt(var + LN_EPS) * g.astype(jnp.float32) + b.astype(jnp.float32)
    return y.astype(x.dtype)


def _swiglu(x, w_in, w_out):
    h = x @ w_in
    gate, up = jnp.split(h, 2, axis=-1)
    return (jax.nn.silu(gate) * up) @ w_out


def _complex_affine_combine(e1, e2):
    a1r, a1i, b1r, b1i = e1
    a2r, a2i, b2r, b2i = e2
    ar = a2r * a1r - a2i * a1i
    ai = a2r * a1i + a2i * a1r
    br = a2r * b1r - a2i * b1i + b2r
    bi = a2r * b1i + a2i * b1r + b2i
    return (ar, ai, br, bi)


def _s5_branch(u, lam_re, lam_im, log_dt, b_re, b_im, c_re, c_im, d_skip, glu_w, glu_b):
    bsz, seq, _ = u.shape
    f32 = jnp.float32
    ug = u.reshape(bsz, seq, SSM_GROUPS, SSM_GROUP_CH).astype(f32)
    dt = jnp.exp(log_dt.astype(f32))[:, None]
    lre = lam_re.astype(f32)
    lim = lam_im.astype(f32)
    mag = jnp.exp(lre * dt)
    ab_re = mag * jnp.cos(lim * dt)
    ab_im = mag * jnp.sin(lim * dt)
    nr = ab_re - 1.0
    ni = ab_im
    den = lre * lre + lim * lim
    coef_re = ((nr * lre + ni * lim) / den)[..., None]
    coef_im = ((ni * lre - nr * lim) / den)[..., None]
    bre = b_re.astype(f32)
    bim = b_im.astype(f32)
    bb_re = coef_re * bre - coef_im * bim
    bb_im = coef_re * bim + coef_im * bre
    bu_re = jnp.einsum('blgi,gpi->blgp', ug, bb_re)
    bu_im = jnp.einsum('blgi,gpi->blgp', ug, bb_im)
    a_re = jnp.broadcast_to(ab_re, bu_re.shape)
    a_im = jnp.broadcast_to(ab_im, bu_im.shape)
    _, _, h_re, h_im = lax.associative_scan(
        _complex_affine_combine, (a_re, a_im, bu_re, bu_im), axis=1)
    y = (jnp.einsum('gip,blgp->blgi', c_re.astype(f32), h_re)
         - jnp.einsum('gip,blgp->blgi', c_im.astype(f32), h_im))
    y = y.reshape(bsz, seq, D_SSM).astype(u.dtype) + d_skip * u
    y = jax.nn.gelu(y)
    return y * jax.nn.sigmoid(y @ glu_w + glu_b)


def _gmlp_branch(z_u, z_v, ln_g, ln_b, w_s, b_s):
    bsz, seq, _ = z_v.shape
    n_chunks = seq // CHUNK
    u = jax.nn.gelu(z_u)
    v = _layer_norm(jax.nn.gelu(z_v), ln_g, ln_b)
    vh = v.reshape(bsz, n_chunks, CHUNK, GMLP_HEADS, GMLP_HEAD_DIM)
    causal = jnp.tril(jnp.ones((CHUNK, CHUNK), dtype=bool))
    ws = jnp.where(causal[None], w_s, 0.0)
    s = jnp.einsum('hts,bcshd->bcthd', ws, vh) + b_s.T[:, :, None]
    return u * s.reshape(bsz, seq, D_GMLP)


def setup_inputs(seed: int = 0) -> dict:
    key = jax.random.key(seed)
    ks = jax.random.split(key, 40)
    f32 = jnp.float32

    def nrm(k, shape, scale):
        return jax.random.normal(k, shape, f32) * scale

    def gain(k, shape):
        return 1.0 + 0.02 * jax.random.normal(k, shape, f32)

    def bias(k, shape):
        return 0.02 * jax.random.normal(k, shape, f32)

    L = DEPTH
    x = jax.random.normal(ks[0], (BATCH, SEQ, D_MODEL), f32)
    p = jax.random.normal(ks[1], (DEPTH, BATCH, SEQ, PLE_DIM), f32)

    ffn1_w_in = nrm(ks[2], (L, D_MODEL, 2 * D_FF), D_MODEL ** -0.5)
    ffn1_w_out = nrm(ks[3], (L, D_FF, D_MODEL), D_FF ** -0.5 * DEEPNORM_BETA)
    ln1_g = gain(ks[4], (L, D_MODEL))
    ln1_b = bias(ks[5], (L, D_MODEL))

    mix_w_in = nrm(ks[6], (L, D_MODEL, D_IN), D_MODEL ** -0.5)
    n_idx = jnp.arange(SSM_STATE, dtype=f32)
    ssm_lambda_re = -0.5 + 0.01 * jax.random.normal(ks[7], (L, SSM_GROUPS, SSM_STATE), f32)
    ssm_lambda_im = (math.pi * n_idx)[None, None, :] + 0.01 * jax.random.normal(
        ks[8], (L, SSM_GROUPS, SSM_STATE), f32)
    ssm_log_dt = jax.random.uniform(ks[9], (L, SSM_GROUPS), f32,
                                    minval=math.log(1e-3), maxval=math.log(1e-1))
    b_scale = (2.0 * SSM_GROUP_CH) ** -0.5
    c_scale = (2.0 * SSM_STATE) ** -0.5
    ssm_b_re = nrm(ks[10], (L, SSM_GROUPS, SSM_STATE, SSM_GROUP_CH), b_scale)
    ssm_b_im = nrm(ks[11], (L, SSM_GROUPS, SSM_STATE, SSM_GROUP_CH), b_scale)
    ssm_c_re = nrm(ks[12], (L, SSM_GROUPS, SSM_GROUP_CH, SSM_STATE), c_scale)
    ssm_c_im = nrm(ks[13], (L, SSM_GROUPS, SSM_GROUP_CH, SSM_STATE), c_scale)
    ssm_d = nrm(ks[14], (L, D_SSM), 1.0)
    ssm_glu_w = nrm(ks[15], (L, D_SSM, D_SSM), D_SSM ** -0.5)
    ssm_glu_b = bias(ks[16], (L, D_SSM))

    gmlp_ln_g = gain(ks[17], (L, D_GMLP))
    gmlp_ln_b = bias(ks[18], (L, D_GMLP))
    gmlp_w_s = nrm(ks[19], (L, GMLP_HEADS, CHUNK, CHUNK), CHUNK ** -0.5)
    gmlp_b_s = 1.0 + 0.02 * jax.random.normal(ks[20], (L, GMLP_HEADS, CHUNK), f32)

    up_a = nrm(ks[21], (L, D_SSM, D_MODEL), D_SSM ** -0.5)
    up_b = nrm(ks[22], (L, D_GMLP, D_MODEL), D_GMLP ** -0.5)
    mix_w_out = nrm(ks[23], (L, D_MODEL, D_MODEL), D_MODEL ** -0.5 * DEEPNORM_BETA)
    ln2_g = gain(ks[24], (L, D_MODEL))
    ln2_b = bias(ks[25], (L, D_MODEL))

    ffn2_w_in = nrm(ks[26], (L, D_MODEL, 2 * D_FF), D_MODEL ** -0.5)
    ffn2_w_out = nrm(ks[27], (L, D_FF, D_MODEL), D_FF ** -0.5 * DEEPNORM_BETA)
    ln3_g = gain(ks[28], (L, D_MODEL))
    ln3_b = bias(ks[29], (L, D_MODEL))

    ple_w_proj = nrm(ks[30], (L, PLE_DIM, D_MODEL), PLE_DIM ** -0.5 * DEEPNORM_BETA)
    ple_w_gate = nrm(ks[31], (L, D_MODEL, D_MODEL), D_MODEL ** -0.5)

    return {
        "x": x, "p": p,
        "ffn1_w_in": ffn1_w_in, "ffn1_w_out": ffn1_w_out, "ln1_g": ln1_g, "ln1_b": ln1_b,
        "mix_w_in": mix_w_in,
        "ssm_lambda_re": ssm_lambda_re, "ssm_lambda_im": ssm_lambda_im, "ssm_log_dt": ssm_log_dt,
        "ssm_b_re": ssm_b_re, "ssm_b_im": ssm_b_im, "ssm_c_re": ssm_c_re, "ssm_c_im": ssm_c_im,
        "ssm_d": ssm_d, "ssm_glu_w": ssm_glu_w, "ssm_glu_b": ssm_glu_b,
        "gmlp_ln_g": gmlp_ln_g, "gmlp_ln_b": gmlp_ln_b, "gmlp_w_s": gmlp_w_s, "gmlp_b_s": gmlp_b_s,
        "up_a": up_a, "up_b": up_b, "mix_w_out": mix_w_out, "ln2_g": ln2_g, "ln2_b": ln2_b,
        "ffn2_w_in": ffn2_w_in, "ffn2_w_out": ffn2_w_out, "ln3_g": ln3_g, "ln3_b": ln3_b,
        "ple_w_proj": ple_w_proj, "ple_w_gate": ple_w_gate,
    }


def reference(x, p, ffn1_w_in, ffn1_w_out, ln1_g, ln1_b, mix_w_in,
              ssm_lambda_re, ssm_lambda_im, ssm_log_dt, ssm_b_re, ssm_b_im, ssm_c_re, ssm_c_im,
              ssm_d, ssm_glu_w, ssm_glu_b, gmlp_ln_g, gmlp_ln_b, gmlp_w_s, gmlp_b_s,
              up_a, up_b, mix_w_out, ln2_g, ln2_b, ffn2_w_in, ffn2_w_out, ln3_g, ln3_b,
              ple_w_proj, ple_w_gate):
    splits = [D_SSM, D_SSM + D_GMLP, D_SSM + 2 * D_GMLP, D_SSM + 2 * D_GMLP + D_MODEL]
    for i in range(DEPTH):
        x = _layer_norm(DEEPNORM_ALPHA * x + 0.5 * _swiglu(x, ffn1_w_in[i], ffn1_w_out[i]),
                        ln1_g[i], ln1_b[i])
        proj = x @ mix_w_in[i]
        z_a, z_u, z_v, g_a, g_b = jnp.split(proj, splits, axis=-1)
        y_a = _s5_branch(z_a, ssm_lambda_re[i], ssm_lambda_im[i], ssm_log_dt[i],
                         ssm_b_re[i], ssm_b_im[i], ssm_c_re[i], ssm_c_im[i],
                         ssm_d[i], ssm_glu_w[i], ssm_glu_b[i]) @ up_a[i]
        y_b = _gmlp_branch(z_u, z_v, gmlp_ln_g[i], gmlp_ln_b[i],
                           gmlp_w_s[i], gmlp_b_s[i]) @ up_b[i]
        mixed = (jax.nn.sigmoid(g_a) * y_a + jax.nn.sigmoid(g_b) * y_b) @ mix_w_out[i]
        x = _layer_norm(DEEPNORM_ALPHA * x + mixed, ln2_g[i], ln2_b[i])
        x = _layer_norm(DEEPNORM_ALPHA * x + 0.5 * _swiglu(x, ffn2_w_in[i], ffn2_w_out[i]),
                        ln3_g[i], ln3_b[i])
        x = x + jax.nn.sigmoid(x @ ple_w_gate[i]) * (p[i] @ ple_w_proj[i])
    return x
```

```python
import functools
import math

import jax
import jax.numpy as jnp
from jax import lax
from jax.experimental import pallas as pl
from jax.experimental.pallas import tpu as pltpu

LN_EPS = 1e-5
CHUNK = 128
LANES = 128
SUBLANES = 8
VMEM_LIMIT_BYTES = 56 * 1024 * 1024

F32 = jnp.float32
BF16 = jnp.bfloat16


def _dot(a, b):
    return jnp.dot(a, b, preferred_element_type=F32)


def _layer_norm(y, g, b):
    mu = jnp.mean(y, axis=-1, keepdims=True)
    d = y - mu
    var = jnp.mean(d * d, axis=-1, keepdims=True)
    return d * lax.rsqrt(var + LN_EPS) * g + b


def _const_spec(shape):
    nd = len(shape)
    return pl.BlockSpec(shape, lambda *_: (0,) * nd, pipeline_mode=pl.Buffered(1))


def _swiglu(xb, w_in_ref, w_out_ref, n_chunks):
    acc = None
    for j in range(n_chunks):
        gate = _dot(xb, w_in_ref[j])
        up = _dot(xb, w_in_ref[n_chunks + j])
        act = (gate * jax.nn.sigmoid(gate) * up).astype(BF16)
        part = _dot(act, w_out_ref[j])
        acc = part if acc is None else acc + part
    return acc


def _ffn_ln_kernel(x_ref, w_in_ref, w_out_ref, g_ref, b_ref, o_ref, *, alpha, n_chunks):
    x = x_ref[...]
    ff = _swiglu(x.astype(BF16), w_in_ref, w_out_ref, n_chunks)
    o_ref[...] = _layer_norm(alpha * x + 0.5 * ff, g_ref[...], b_ref[...])


def _ffn_ln_ple_kernel(x_ref, p_ref, w_in_ref, w_out_ref, g_ref, b_ref, wg_ref, wp_ref,
                       o_ref, *, alpha, n_chunks):
    x = x_ref[...]
    ff = _swiglu(x.astype(BF16), w_in_ref, w_out_ref, n_chunks)
    x3 = _layer_norm(alpha * x + 0.5 * ff, g_ref[...], b_ref[...])
    gate = jax.nn.sigmoid(_dot(x3.astype(BF16), wg_ref[...]))
    emb = _dot(p_ref[...].astype(BF16), wp_ref[...])
    o_ref[...] = x3 + gate * emb


def _ffn_chunk(d_ff):
    for fc in (256, 128):
        if d_ff % fc == 0:
            return fc
    raise ValueError(f"d_ff={d_ff} must be a multiple of {LANES}")


def _prep_ffn_weights(w_in, w_out):
    d, two_ff = w_in.shape
    d_ff = two_ff // 2
    fc = _ffn_chunk(d_ff)
    n_chunks = d_ff // fc
    w_in_c = w_in.astype(BF16).reshape(d, 2 * n_chunks, fc).transpose(1, 0, 2)
    w_out_c = w_out.astype(BF16).reshape(n_chunks, fc, d)
    return w_in_c, w_out_c, n_chunks


def _ffn_ln(x2d, w_in, w_out, g, b, alpha, tm, ple=None):
    n, d = x2d.shape
    w_in_c, w_out_c, n_chunks = _prep_ffn_weights(w_in, w_out)
    row = lambda i: (i, 0)
    in_specs = [pl.BlockSpec((tm, d), row)]
    args = [x2d]
    if ple is not None:
        p2d, w_gate, w_proj = ple
        in_specs.append(pl.BlockSpec((tm, p2d.shape[1]), row))
        args.append(p2d)
    in_specs += [_const_spec(w_in_c.shape), _const_spec(w_out_c.shape),
                 _const_spec((1, d)), _const_spec((1, d))]
    args += [w_in_c, w_out_c, g.reshape(1, d), b.reshape(1, d)]
    if ple is None:
        body = functools.partial(_ffn_ln_kernel, alpha=alpha, n_chunks=n_chunks)
        name = "ffn_ln"
    else:
        in_specs += [_const_spec(w_gate.shape), _const_spec(w_proj.shape)]
        args += [w_gate.astype(BF16), w_proj.astype(BF16)]
        body = functools.partial(_ffn_ln_ple_kernel, alpha=alpha, n_chunks=n_chunks)
        name = "ffn_ln_ple"
    return pl.pallas_call(
        body,
        grid=(n // tm,),
        in_specs=in_specs,
        out_specs=pl.BlockSpec((tm, d), row),
        out_shape=jax.ShapeDtypeStruct((n, d), F32),
        compiler_params=pltpu.CompilerParams(
            dimension_semantics=("arbitrary",), vmem_limit_bytes=VMEM_LIMIT_BYTES),
        name=name,
    )(*args)


def _s5_kernel(x_ref, wa_ref, bblk_ref, cblk_ref, are_ref, aim_ref, dskip_ref,
               gluw_ref, glub_ref, o_ref, za_t_ref, bu_ref, y_t_ref, hre_ref, him_ref,
               *, nb, tt, n_half):
    rows = nb * tt
    d = x_ref.shape[-1]
    hw = bu_ref.shape[1] // 2
    slabs_per_half = za_t_ref.shape[0] // n_half

    @pl.when(pl.program_id(1) == 0)
    def _():
        hre_ref[...] = jnp.zeros_like(hre_ref)
        him_ref[...] = jnp.zeros_like(him_ref)

    xb = x_ref[...].reshape(rows, d).astype(BF16)
    za = _dot(xb, wa_ref[...])

    for k in range(za_t_ref.shape[0]):
        for bi in range(nb):
            za_t_ref[k, pl.ds(bi, tt, stride=nb), :] = (
                za[bi * tt:(bi + 1) * tt, k * LANES:(k + 1) * LANES])

    for hf in range(n_half):
        u_half = jnp.concatenate(
            [za_t_ref[hf * slabs_per_half + k] for k in range(slabs_per_half)], axis=1)
        bu_ref[...] = _dot(u_half.astype(BF16), bblk_ref[hf])

        a_re = jnp.broadcast_to(are_ref[hf], (nb, hw))
        a_im = jnp.broadcast_to(aim_ref[hf], (nb, hw))

        def step(s, carry):
            h_re, h_im = carry
            r0 = pl.multiple_of(s * nb, nb)
            b_re = bu_ref[pl.ds(r0, nb), 0:hw]
            b_im = bu_ref[pl.ds(r0, nb), hw:2 * hw]
            n_re = a_re * h_re - a_im * h_im + b_re
            n_im = a_re * h_im + a_im * h_re + b_im
            bu_ref[pl.ds(r0, nb), 0:hw] = n_re
            bu_ref[pl.ds(r0, nb), hw:2 * hw] = n_im
            return n_re, n_im

        h_re, h_im = lax.fori_loop(0, tt, step, (hre_ref[hf], him_ref[hf]), unroll=4)
        hre_ref[hf] = h_re
        him_ref[hf] = h_im

        y_half = _dot(bu_ref[...].astype(BF16), cblk_ref[hf])
        for k in range(slabs_per_half):
            y_t_ref[hf * slabs_per_half + k] = y_half[:, k * LANES:(k + 1) * LANES]

    y = jnp.concatenate(
        [jnp.concatenate([y_t_ref[k, pl.ds(bi, tt, stride=nb), :] for bi in range(nb)], axis=0)
         for k in range(y_t_ref.shape[0])], axis=1)
    y = jax.nn.gelu(y + dskip_ref[...] * za)
    glu = jax.nn.sigmoid(_dot(y.astype(BF16), gluw_ref[...]) + glub_ref[...])
    o_ref[...] = (y * glu).reshape(nb, tt, -1)


def _discretise(lam_re, lam_im, log_dt, b_re, b_im):
    dt = jnp.exp(log_dt.astype(F32))[:, None]
    lre = lam_re.astype(F32)
    lim = lam_im.astype(F32)
    mag = jnp.exp(lre * dt)
    ab_re = mag * jnp.cos(lim * dt)
    ab_im = mag * jnp.sin(lim * dt)
    nr = ab_re - 1.0
    ni = ab_im
    den = lre * lre + lim * lim
    coef_re = ((nr * lre + ni * lim) / den)[..., None]
    coef_im = ((ni * lre - nr * lim) / den)[..., None]
    bre = b_re.astype(F32)
    bim = b_im.astype(F32)
    bb_re = coef_re * bre - coef_im * bim
    bb_im = coef_re * bim + coef_im * bre
    return ab_re, ab_im, bb_re, bb_im


def _block_diag_ssm(ab_re, ab_im, bb_re, bb_im, c_re, c_im, n_half):
    g, p, ch = bb_re.shape
    gh = g // n_half
    eye = jnp.eye(gh, dtype=F32)

    def in_map(bb):
        return jnp.einsum('gpi,gk->gikp', bb, eye).reshape(gh * ch, gh * p)

    def out_map(c):
        return jnp.einsum('gip,gk->gpki', c, eye).reshape(gh * p, gh * ch)

    bblk, cblk = [], []
    for h in range(n_half):
        sl = slice(h * gh, (h + 1) * gh)
        bblk.append(jnp.concatenate([in_map(bb_re[sl]), in_map(bb_im[sl])], axis=1))
        cblk.append(jnp.concatenate([out_map(c_re[sl].astype(F32)),
                                     -out_map(c_im[sl].astype(F32))], axis=0))
    a_re = ab_re.reshape(n_half, 1, gh * p)
    a_im = ab_im.reshape(n_half, 1, gh * p)
    return jnp.stack(bblk).astype(BF16), jnp.stack(cblk).astype(BF16), a_re, a_im


def _s5_branch(x1, w_a, lam_re, lam_im, log_dt, b_re, b_im, c_re, c_im, d_skip, glu_w, glu_b):
    bsz, seq, d = x1.shape
    d_ssm = w_a.shape[1]
    nb, tt, n_half = SUBLANES, CHUNK, 2
    ab_re, ab_im, bb_re, bb_im = _discretise(lam_re, lam_im, log_dt, b_re, b_im)
    bblk, cblk, a_re, a_im = _block_diag_ssm(ab_re, ab_im, bb_re, bb_im, c_re, c_im, n_half)
    hw = a_re.shape[-1]
    rows = nb * tt
    n_slabs = d_ssm // LANES
    body = functools.partial(_s5_kernel, nb=nb, tt=tt, n_half=n_half)
    return pl.pallas_call(
        body,
        grid=(bsz // nb, seq // tt),
        in_specs=[
            pl.BlockSpec((nb, tt, d), lambda i, j: (i, j, 0)),
            _const_spec(w_a.shape), _const_spec(bblk.shape), _const_spec(cblk.shape),
            _const_spec(a_re.shape), _const_spec(a_im.shape),
            _const_spec((1, d_ssm)), _const_spec(glu_w.shape), _const_spec((1, d_ssm)),
        ],
        out_specs=pl.BlockSpec((nb, tt, d_ssm), lambda i, j: (i, j, 0)),
        out_shape=jax.ShapeDtypeStruct((bsz, seq, d_ssm), F32),
        scratch_shapes=[
            pltpu.VMEM((n_slabs, rows, LANES), F32),
            pltpu.VMEM((rows, 2 * hw), F32),
            pltpu.VMEM((n_slabs, rows, LANES), F32),
            pltpu.VMEM((n_half, nb, hw), F32),
            pltpu.VMEM((n_half, nb, hw), F32),
        ],
        compiler_params=pltpu.CompilerParams(
            dimension_semantics=("arbitrary", "arbitrary"),
            vmem_limit_bytes=VMEM_LIMIT_BYTES),
        name="s5_branch",
    )(x1, w_a.astype(BF16), bblk, cblk, a_re, a_im, d_skip.reshape(1, d_ssm).astype(F32),
      glu_w.astype(BF16), glu_b.reshape(1, d_ssm).astype(F32))


def _mixer_kernel(x_ref, ya_ref, w_ref, lng_ref, lnb_ref, ws_ref, bs_ref, ua_ref, ub_ref,
                  wo_ref, g_ref, b_ref, o_ref, *, alpha, d_g, n_pairs):
    x = x_ref[...]
    tm, d = x.shape
    xb = x.astype(BF16)
    u = jax.nn.gelu(_dot(xb, w_ref[:, 0:d_g]))
    v = _layer_norm(jax.nn.gelu(_dot(xb, w_ref[:, d_g:2 * d_g])), lng_ref[...], lnb_ref[...])
    vb = v.astype(BF16)

    lane = lax.broadcasted_iota(jnp.int32, (CHUNK, LANES), 1)
    first_head = lane < (LANES // 2)
    zero = jnp.zeros((CHUNK, LANES), BF16)
    chunks = []
    for c in range(tm // CHUNK):
        pairs = []
        for hp in range(n_pairs):
            vp = vb[c * CHUNK:(c + 1) * CHUNK, hp * LANES:(hp + 1) * LANES]
            rhs = jnp.concatenate([jnp.where(first_head, vp, zero),
                                   jnp.where(first_head, zero, vp)], axis=0)
            pairs.append(_dot(ws_ref[hp], rhs))
        chunks.append(jnp.concatenate(pairs, axis=1) + bs_ref[...])
    s = jnp.concatenate(chunks, axis=0)
    yb = _dot((u * s).astype(BF16), ub_ref[...])
    ya = _dot(ya_ref[...].astype(BF16), ua_ref[...])
    ga = jax.nn.sigmoid(_dot(xb, w_ref[:, 2 * d_g:2 * d_g + d]))
    gb = jax.nn.sigmoid(_dot(xb, w_ref[:, 2 * d_g + d:2 * d_g + 2 * d]))
    mixed = _dot((ga * ya + gb * yb).astype(BF16), wo_ref[...])
    o_ref[...] = _layer_norm(alpha * x + mixed, g_ref[...], b_ref[...])


def _mixer(x1_2d, ya_2d, w_rest, ln_g, ln_b, w_s, b_s, up_a, up_b, w_out, g, b, alpha, tm):
    n, d = x1_2d.shape
    d_g = up_b.shape[0]
    heads, chunk, _ = w_s.shape
    head_dim = d_g // heads
    assert chunk == CHUNK and 2 * head_dim == LANES and heads % 2 == 0
    n_pairs = heads // 2
    causal = jnp.tril(jnp.ones((chunk, chunk), dtype=bool))
    ws = jnp.where(causal[None], w_s, 0.0).astype(BF16)
    ws_pairs = ws.reshape(n_pairs, 2, chunk, chunk).transpose(0, 2, 1, 3).reshape(
        n_pairs, chunk, 2 * chunk)
    bs_lanes = jnp.repeat(b_s.T.astype(F32), head_dim, axis=1)
    row = lambda i: (i, 0)
    body = functools.partial(_mixer_kernel, alpha=alpha, d_g=d_g, n_pairs=n_pairs)
    return pl.pallas_call(
        body,
        grid=(n // tm,),
        in_specs=[
            pl.BlockSpec((tm, d), row), pl.BlockSpec((tm, ya_2d.shape[1]), row),
            _const_spec(w_rest.shape), _const_spec((1, d_g)), _const_spec((1, d_g)),
            _const_spec(ws_pairs.shape), _const_spec(bs_lanes.shape),
            _const_spec(up_a.shape), _const_spec(up_b.shape), _const_spec(w_out.shape),
            _const_spec((1, d)), _const_spec((1, d)),
        ],
        out_specs=pl.BlockSpec((tm, d), row),
        out_shape=jax.ShapeDtypeStruct((n, d), F32),
        compiler_params=pltpu.CompilerParams(
            dimension_semantics=("arbitrary",), vmem_limit_bytes=VMEM_LIMIT_BYTES),
        name="mixer",
    )(x1_2d, ya_2d, w_rest.astype(BF16), ln_g.reshape(1, d_g), ln_b.reshape(1, d_g),
      ws_pairs, bs_lanes, up_a.astype(BF16), up_b.astype(BF16), w_out.astype(BF16),
      g.reshape(1, d), b.reshape(1, d))


def kernel(x, p, ffn1_w_in, ffn1_w_out, ln1_g, ln1_b, mix_w_in, ssm_lambda_re, ssm_lambda_im, ssm_log_dt, ssm_b_re, ssm_b_im, ssm_c_re, ssm_c_im, ssm_d, ssm_glu_w, ssm_glu_b, gmlp_ln_g, gmlp_ln_b, gmlp_w_s, gmlp_b_s, up_a, up_b, mix_w_out, ln2_g, ln2_b, ffn2_w_in, ffn2_w_out, ln3_g, ln3_b, ple_w_proj, ple_w_gate):
    bsz, seq, d = x.shape
    depth = ffn1_w_in.shape[0]
    d_ssm = ssm_d.shape[1]
    alpha = (2.0 * depth) ** 0.25
    n = bsz * seq
    tm_ffn = 512
    tm_mix = 512
    assert seq % tm_mix == 0 and tm_mix % CHUNK == 0 and n % tm_ffn == 0
    assert bsz % SUBLANES == 0 and seq % CHUNK == 0 and d_ssm % (2 * LANES) == 0

    x2d = x.reshape(n, d)
    for i in range(depth):
        x1 = _ffn_ln(x2d, ffn1_w_in[i], ffn1_w_out[i], ln1_g[i], ln1_b[i], alpha, tm_ffn)
        ya = _s5_branch(x1.reshape(bsz, seq, d), mix_w_in[i][:, :d_ssm],
                        ssm_lambda_re[i], ssm_lambda_im[i], ssm_log_dt[i],
                        ssm_b_re[i], ssm_b_im[i], ssm_c_re[i], ssm_c_im[i],
                        ssm_d[i], ssm_glu_w[i], ssm_glu_b[i])
        x2 = _mixer(x1, ya.reshape(n, d_ssm), mix_w_in[i][:, d_ssm:],
                    gmlp_ln_g[i], gmlp_ln_b[i], gmlp_w_s[i], gmlp_b_s[i],
                    up_a[i], up_b[i], mix_w_out[i], ln2_g[i], ln2_b[i], alpha, tm_mix)
        x2d = _ffn_ln(x2, ffn2_w_in[i], ffn2_w_out[i], ln3_g[i], ln3_b[i], alpha, tm_ffn,
                      ple=(p[i].reshape(n, -1), ple_w_gate[i], ple_w_proj[i]))
    return x2d.reshape(bsz, seq, d)
```

```python
import functools
import math

import jax
import jax.numpy as jnp
from jax import lax
from jax.experimental import pallas as pl
from jax.experimental.pallas import tpu as pltpu

LN_EPS = 1e-5
CHUNK = 128
LANES = 128
SUBLANES = 8
VMEM_LIMIT_BYTES = 56 * 1024 * 1024

F32 = jnp.float32
BF16 = jnp.bfloat16


def _dot(a, b):
    return jnp.dot(a, b, preferred_element_type=F32)


def _layer_norm(y, g, b):
    mu = jnp.mean(y, axis=-1, keepdims=True)
    d = y - mu
    var = jnp.mean(d * d, axis=-1, keepdims=True)
    return d * lax.rsqrt(var + LN_EPS) * g + b


def _const_spec(shape):
    nd = len(shape)
    return pl.BlockSpec(shape, lambda *_: (0,) * nd, pipeline_mode=pl.Buffered(1))


def _swiglu(xb, w_in_ref, w_out_ref, n_chunks):
    d_ff = w_out_ref.shape[0]
    fc = d_ff // n_chunks
    acc = None
    for j in range(n_chunks):
        gate = _dot(xb, w_in_ref[:, j * fc:(j + 1) * fc])
        up = _dot(xb, w_in_ref[:, d_ff + j * fc:d_ff + (j + 1) * fc])
        act = (gate * jax.nn.sigmoid(gate) * up).astype(BF16)
        part = _dot(act, w_out_ref[j * fc:(j + 1) * fc, :])
        acc = part if acc is None else acc + part
    return acc


def _ffn_ln_kernel(x_ref, w_in_ref, w_out_ref, g_ref, b_ref, o_ref, *, alpha, n_chunks):
    x = x_ref[...]
    ff = _swiglu(x.astype(BF16), w_in_ref, w_out_ref, n_chunks)
    o_ref[...] = _layer_norm(alpha * x + 0.5 * ff, g_ref[...], b_ref[...])


def _ffn_ln_ple_kernel(x_ref, p_ref, w_in_ref, w_out_ref, g_ref, b_ref, wg_ref, wp_ref,
                       o_ref, *, alpha, n_chunks):
    x = x_ref[...]
    ff = _swiglu(x.astype(BF16), w_in_ref, w_out_ref, n_chunks)
    x3 = _layer_norm(alpha * x + 0.5 * ff, g_ref[...], b_ref[...])
    gate = jax.nn.sigmoid(_dot(x3.astype(BF16), wg_ref[...]))
    emb = _dot(p_ref[...].astype(BF16), wp_ref[...])
    o_ref[...] = x3 + gate * emb


def _ffn_chunk(d_ff):
    for fc in (256, 128):
        if d_ff % fc == 0:
            return fc
    raise ValueError(f"d_ff={d_ff} must be a multiple of {LANES}")


def _prep_ffn_weights(w_in, w_out):
    d_ff = w_out.shape[0]
    n_chunks = d_ff // _ffn_chunk(d_ff)
    return w_in.astype(BF16), w_out.astype(BF16), n_chunks


def _ffn_ln(x2d, w_in, w_out, g, b, alpha, tm, ple=None):
    n, d = x2d.shape
    w_in_c, w_out_c, n_chunks = _prep_ffn_weights(w_in, w_out)
    row = lambda i: (i, 0)
    in_specs = [pl.BlockSpec((tm, d), row)]
    args = [x2d]
    if ple is not None:
        p2d, w_gate, w_proj = ple
        in_specs.append(pl.BlockSpec((tm, p2d.shape[1]), row))
        args.append(p2d)
    in_specs += [_const_spec(w_in_c.shape), _const_spec(w_out_c.shape),
                 _const_spec((1, d)), _const_spec((1, d))]
    args += [w_in_c, w_out_c, g.reshape(1, d), b.reshape(1, d)]
    if ple is None:
        body = functools.partial(_ffn_ln_kernel, alpha=alpha, n_chunks=n_chunks)
        name = "ffn_ln"
    else:
        in_specs += [_const_spec(w_gate.shape), _const_spec(w_proj.shape)]
        args += [w_gate.astype(BF16), w_proj.astype(BF16)]
        body = functools.partial(_ffn_ln_ple_kernel, alpha=alpha, n_chunks=n_chunks)
        name = "ffn_ln_ple"
    return pl.pallas_call(
        body,
        grid=(n // tm,),
        in_specs=in_specs,
        out_specs=pl.BlockSpec((tm, d), row),
        out_shape=jax.ShapeDtypeStruct((n, d), F32),
        compiler_params=pltpu.CompilerParams(
            dimension_semantics=("arbitrary",), vmem_limit_bytes=VMEM_LIMIT_BYTES),
        name=name,
    )(*args)


def _s5_kernel(x_ref, wa_ref, bblk_ref, cblk_ref, are_ref, aim_ref, dskip_ref,
               gluw_ref, glub_ref, o_ref, za_t_ref, bu_ref, y_t_ref, hre_ref, him_ref,
               *, nb, tt, n_half):
    rows = nb * tt
    d = x_ref.shape[-1]
    hw = bu_ref.shape[2] // 2
    slabs_per_half = za_t_ref.shape[0] // n_half

    @pl.when(pl.program_id(1) == 0)
    def _():
        hre_ref[...] = jnp.zeros_like(hre_ref)
        him_ref[...] = jnp.zeros_like(him_ref)

    xb = x_ref[...].reshape(rows, d).astype(BF16)
    za = _dot(xb, wa_ref[...])

    for k in range(za_t_ref.shape[0]):
        for bi in range(nb):
            za_t_ref[k, pl.ds(bi, tt, stride=nb), :] = (
                za[bi * tt:(bi + 1) * tt, k * LANES:(k + 1) * LANES])

    for hf in range(n_half):
        u_half = jnp.concatenate(
            [za_t_ref[hf * slabs_per_half + k] for k in range(slabs_per_half)], axis=1)
        bu_ref[hf] = _dot(u_half.astype(BF16), bblk_ref[hf])

    for hf in range(n_half):
        a_re = jnp.broadcast_to(are_ref[hf], (nb, hw))
        a_im = jnp.broadcast_to(aim_ref[hf], (nb, hw))
        h_re = hre_ref[hf]
        h_im = him_ref[hf]
        for s in range(tt):
            r = slice(s * nb, (s + 1) * nb)
            n_re = a_re * h_re - a_im * h_im + bu_ref[hf, r, 0:hw]
            n_im = a_re * h_im + a_im * h_re + bu_ref[hf, r, hw:2 * hw]
            bu_ref[hf, r, 0:hw] = n_re
            bu_ref[hf, r, hw:2 * hw] = n_im
            h_re, h_im = n_re, n_im
        hre_ref[hf] = h_re
        him_ref[hf] = h_im

    for hf in range(n_half):
        y_half = _dot(bu_ref[hf].astype(BF16), cblk_ref[hf])
        for k in range(slabs_per_half):
            y_t_ref[hf * slabs_per_half + k] = y_half[:, k * LANES:(k + 1) * LANES]

    y = jnp.concatenate(
        [jnp.concatenate([y_t_ref[k, pl.ds(bi, tt, stride=nb), :] for bi in range(nb)], axis=0)
         for k in range(y_t_ref.shape[0])], axis=1)
    y = jax.nn.gelu(y + dskip_ref[...] * za)
    glu = jax.nn.sigmoid(_dot(y.astype(BF16), gluw_ref[...]) + glub_ref[...])
    o_ref[...] = (y * glu).reshape(nb, tt, -1)


def _discretise(lam_re, lam_im, log_dt, b_re, b_im):
    dt = jnp.exp(log_dt.astype(F32))[:, None]
    lre = lam_re.astype(F32)
    lim = lam_im.astype(F32)
    mag = jnp.exp(lre * dt)
    ab_re = mag * jnp.cos(lim * dt)
    ab_im = mag * jnp.sin(lim * dt)
    nr = ab_re - 1.0
    ni = ab_im
    den = lre * lre + lim * lim
    coef_re = ((nr * lre + ni * lim) / den)[..., None]
    coef_im = ((ni * lre - nr * lim) / den)[..., None]
    bre = b_re.astype(F32)
    bim = b_im.astype(F32)
    bb_re = coef_re * bre - coef_im * bim
    bb_im = coef_re * bim + coef_im * bre
    return ab_re, ab_im, bb_re, bb_im


def _block_diag_ssm(ab_re, ab_im, bb_re, bb_im, c_re, c_im, n_half):
    g, p, ch = bb_re.shape
    gh = g // n_half
    eye = jnp.eye(gh, dtype=F32)

    def in_map(bb):
        return jnp.einsum('gpi,gk->gikp', bb, eye).reshape(gh * ch, gh * p)

    def out_map(c):
        return jnp.einsum('gip,gk->gpki', c, eye).reshape(gh * p, gh * ch)

    bblk, cblk = [], []
    for h in range(n_half):
        sl = slice(h * gh, (h + 1) * gh)
        bblk.append(jnp.concatenate([in_map(bb_re[sl]), in_map(bb_im[sl])], axis=1))
        cblk.append(jnp.concatenate([out_map(c_re[sl].astype(F32)),
                                     -out_map(c_im[sl].astype(F32))], axis=0))
    a_re = ab_re.reshape(n_half, 1, gh * p)
    a_im = ab_im.reshape(n_half, 1, gh * p)
    return jnp.stack(bblk).astype(BF16), jnp.stack(cblk).astype(BF16), a_re, a_im


def _s5_branch(x1, w_a, lam_re, lam_im, log_dt, b_re, b_im, c_re, c_im, d_skip, glu_w, glu_b):
    bsz, seq, d = x1.shape
    d_ssm = w_a.shape[1]
    nb, tt, n_half = SUBLANES, CHUNK, 2
    ab_re, ab_im, bb_re, bb_im = _discretise(lam_re, lam_im, log_dt, b_re, b_im)
    bblk, cblk, a_re, a_im = _block_diag_ssm(ab_re, ab_im, bb_re, bb_im, c_re, c_im, n_half)
    hw = a_re.shape[-1]
    rows = nb * tt
    n_slabs = d_ssm // LANES
    body = functools.partial(_s5_kernel, nb=nb, tt=tt, n_half=n_half)
    return pl.pallas_call(
        body,
        grid=(bsz // nb, seq // tt),
        in_specs=[
            pl.BlockSpec((nb, tt, d), lambda i, j: (i, j, 0)),
            _const_spec(w_a.shape), _const_spec(bblk.shape), _const_spec(cblk.shape),
            _const_spec(a_re.shape), _const_spec(a_im.shape),
            _const_spec((1, d_ssm)), _const_spec(glu_w.shape), _const_spec((1, d_ssm)),
        ],
        out_specs=pl.BlockSpec((nb, tt, d_ssm), lambda i, j: (i, j, 0)),
        out_shape=jax.ShapeDtypeStruct((bsz, seq, d_ssm), F32),
        scratch_shapes=[
            pltpu.VMEM((n_slabs, rows, LANES), F32),
            pltpu.VMEM((n_half, rows, 2 * hw), F32),
            pltpu.VMEM((n_slabs, rows, LANES), F32),
            pltpu.VMEM((n_half, nb, hw), F32),
            pltpu.VMEM((n_half, nb, hw), F32),
        ],
        compiler_params=pltpu.CompilerParams(
            dimension_semantics=("arbitrary", "arbitrary"),
            vmem_limit_bytes=VMEM_LIMIT_BYTES),
        name="s5_branch",
    )(x1, w_a.astype(BF16), bblk, cblk, a_re, a_im, d_skip.reshape(1, d_ssm).astype(F32),
      glu_w.astype(BF16), glu_b.reshape(1, d_ssm).astype(F32))


def _mixer_kernel(x_ref, ya_ref, w_ref, lng_ref, lnb_ref, ws_ref, bs_ref, ua_ref, ub_ref,
                  wo_ref, g_ref, b_ref, o_ref, *, alpha, d_g, n_pairs):
    x = x_ref[...]
    tm, d = x.shape
    xb = x.astype(BF16)
    u = jax.nn.gelu(_dot(xb, w_ref[:, 0:d_g]))
    v = _layer_norm(jax.nn.gelu(_dot(xb, w_ref[:, d_g:2 * d_g])), lng_ref[...], lnb_ref[...])
    vb = v.astype(BF16)

    lane = lax.broadcasted_iota(jnp.int32, (CHUNK, LANES), 1)
    first_head = lane < (LANES // 2)
    zero = jnp.zeros((CHUNK, LANES), BF16)
    chunks = []
    for c in range(tm // CHUNK):
        pairs = []
        for hp in range(n_pairs):
            vp = vb[c * CHUNK:(c + 1) * CHUNK, hp * LANES:(hp + 1) * LANES]
            rhs = jnp.concatenate([jnp.where(first_head, vp, zero),
                                   jnp.where(first_head, zero, vp)], axis=0)
            pairs.append(_dot(ws_ref[hp], rhs))
        chunks.append(jnp.concatenate(pairs, axis=1) + bs_ref[...])
    s = jnp.concatenate(chunks, axis=0)
    yb = _dot((u * s).astype(BF16), ub_ref[...])
    ya = _dot(ya_ref[...].astype(BF16), ua_ref[...])
    ga = jax.nn.sigmoid(_dot(xb, w_ref[:, 2 * d_g:2 * d_g + d]))
    gb = jax.nn.sigmoid(_dot(xb, w_ref[:, 2 * d_g + d:2 * d_g + 2 * d]))
    mixed = _dot((ga * ya + gb * yb).astype(BF16), wo_ref[...])
    o_ref[...] = _layer_norm(alpha * x + mixed, g_ref[...], b_ref[...])


def _mixer(x1_2d, ya_2d, w_rest, ln_g, ln_b, w_s, b_s, up_a, up_b, w_out, g, b, alpha, tm):
    n, d = x1_2d.shape
    d_g = up_b.shape[0]
    heads, chunk, _ = w_s.shape
    head_dim = d_g // heads
    assert chunk == CHUNK and 2 * head_dim == LANES and heads % 2 == 0
    n_pairs = heads // 2
    causal = jnp.tril(jnp.ones((chunk, chunk), dtype=bool))
    ws = jnp.where(causal[None], w_s, 0.0).astype(BF16)
    ws_pairs = ws.reshape(n_pairs, 2, chunk, chunk).transpose(0, 2, 1, 3).reshape(
        n_pairs, chunk, 2 * chunk)
    bs_lanes = jnp.repeat(b_s.T.astype(F32), head_dim, axis=1)
    row = lambda i: (i, 0)
    body = functools.partial(_mixer_kernel, alpha=alpha, d_g=d_g, n_pairs=n_pairs)
    return pl.pallas_call(
        body,
        grid=(n // tm,),
        in_specs=[
            pl.BlockSpec((tm, d), row), pl.BlockSpec((tm, ya_2d.shape[1]), row),
            _const_spec(w_rest.shape), _const_spec((1, d_g)), _const_spec((1, d_g)),
            _const_spec(ws_pairs.shape), _const_spec(bs_lanes.shape),
            _const_spec(up_a.shape), _const_spec(up_b.shape), _const_spec(w_out.shape),
            _const_spec((1, d)), _const_spec((1, d)),
        ],
        out_specs=pl.BlockSpec((tm, d), row),
        out_shape=jax.ShapeDtypeStruct((n, d), F32),
        compiler_params=pltpu.CompilerParams(
            dimension_semantics=("arbitrary",), vmem_limit_bytes=VMEM_LIMIT_BYTES),
        name="mixer",
    )(x1_2d, ya_2d, w_rest.astype(BF16), ln_g.reshape(1, d_g), ln_b.reshape(1, d_g),
      ws_pairs, bs_lanes, up_a.astype(BF16), up_b.astype(BF16), w_out.astype(BF16),
      g.reshape(1, d), b.reshape(1, d))


def kernel(x, p, ffn1_w_in, ffn1_w_out, ln1_g, ln1_b, mix_w_in, ssm_lambda_re, ssm_lambda_im, ssm_log_dt, ssm_b_re, ssm_b_im, ssm_c_re, ssm_c_im, ssm_d, ssm_glu_w, ssm_glu_b, gmlp_ln_g, gmlp_ln_b, gmlp_w_s, gmlp_b_s, up_a, up_b, mix_w_out, ln2_g, ln2_b, ffn2_w_in, ffn2_w_out, ln3_g, ln3_b, ple_w_proj, ple_w_gate):
    bsz, seq, d = x.shape
    depth = ffn1_w_in.shape[0]
    d_ssm = ssm_d.shape[1]
    alpha = (2.0 * depth) ** 0.25
    n = bsz * seq
    tm_ffn = 512
    tm_mix = 512
    assert seq % tm_mix == 0 and tm_mix % CHUNK == 0 and n % tm_ffn == 0
    assert bsz % SUBLANES == 0 and seq % CHUNK == 0 and d_ssm % (2 * LANES) == 0

    x2d = x.reshape(n, d)
    for i in range(depth):
        x1 = _ffn_ln(x2d, ffn1_w_in[i], ffn1_w_out[i], ln1_g[i], ln1_b[i], alpha, tm_ffn)
        ya = _s5_branch(x1.reshape(bsz, seq, d), mix_w_in[i][:, :d_ssm],
                        ssm_lambda_re[i], ssm_lambda_im[i], ssm_log_dt[i],
                        ssm_b_re[i], ssm_b_im[i], ssm_c_re[i], ssm_c_im[i],
                        ssm_d[i], ssm_glu_w[i], ssm_glu_b[i])
        x2 = _mixer(x1, ya.reshape(n, d_ssm), mix_w_in[i][:, d_ssm:],
                    gmlp_ln_g[i], gmlp_ln_b[i], gmlp_w_s[i], gmlp_b_s[i],
                    up_a[i], up_b[i], mix_w_out[i], ln2_g[i], ln2_b[i], alpha, tm_mix)
        x2d = _ffn_ln(x2, ffn2_w_in[i], ffn2_w_out[i], ln3_g[i], ln3_b[i], alpha, tm_ffn,
                      ple=(p[i].reshape(n, -1), ple_w_gate[i], ple_w_proj[i]))
    return x2d.reshape(bsz, seq, d)
```
